```python
import math
import jax, jax.numpy as jnp
from jax import lax
import numpy as np

D_MODEL = 1024
BATCH = 16
SEQ = 2048
DEPTH = 2

N_MIXERS = 2
N_POOL_LAYERS = (DEPTH + 1) // 2
N_RG_LAYERS = DEPTH // 2
MEM_LEN = 256
POOL_WINDOWS = (2, 4, 8, 16)
POOL_GROUPS = len(POOL_WINDOWS)
POOL_GW = D_MODEL // POOL_GROUPS
D_RNN = D_MODEL
RG_HEADS = 4
RG_BLOCK = D_RNN // RG_HEADS
CONV_W = 4
RG_C = 8.0
XA_HEADS = 4
XA_HEAD_DIM = D_MODEL // XA_HEADS
N_GROUPS = 4
EXPERTS_PER_GROUP = 4
N_EXPERTS = N_GROUPS * EXPERTS_PER_GROUP
TOP_K_IN_GROUP = 2
D_EXPERT = D_MODEL // 2
RMS_EPS = 1e-6

kernel_name = "hybrid_pool_rglru_memxattn_hmoe"


def _rms_norm(x, g):
    xf = x.astype(jnp.float32)
    y = xf * lax.rsqrt(jnp.mean(xf * xf, axis=-1, keepdims=True) + RMS_EPS)
    return (y * g.astype(jnp.float32)).astype(x.dtype)


def _pool_mixer(xn, w_grp, scale):
    B, S, D = xn.shape
    xf = xn.astype(jnp.float32)
    pos = jnp.arange(1, S + 1, dtype=jnp.int32)
    outs = []
    for g, w in enumerate(POOL_WINDOWS):
        xg = xf[..., g * POOL_GW:(g + 1) * POOL_GW]
        c = jnp.cumsum(xg, axis=1)
        c_shift = jnp.pad(c, ((0, 0), (w, 0), (0, 0)))[:, :S]
        count = jnp.minimum(pos, w).astype(jnp.float32)[None, :, None]
        outs.append((c - c_shift) / count - xg)
    p = jnp.concatenate(outs, axis=-1).astype(xn.dtype).reshape(B, S, POOL_GROUPS, POOL_GW)
    y = jnp.einsum('bsgi,gio->bsgo', p, w_grp).reshape(B, S, D)
    return y * scale


def _causal_depthwise_conv(x, w, b):
    S = x.shape[1]
    xp = jnp.pad(x, ((0, 0), (CONV_W - 1, 0), (0, 0)))
    y = b
    for k in range(CONV_W):
        y = y + xp[:, k:k + S] * w[k]
    return y


def _lru_combine(c1, c2):
    a1, b1 = c1
    a2, b2 = c2
    return a1 * a2, a2 * b1 + b2


def _rglru_block(xn, w_in, conv_w, conv_b, w_gate, b_gate, lam, w_out):
    B, S, _ = xn.shape
    proj = xn @ w_in
    xb, yb = proj[..., :D_RNN], proj[..., D_RNN:]
    gate_branch = jax.nn.gelu(yb, approximate=True)
    xc = _causal_depthwise_conv(xb, conv_w, conv_b)
    xh = xc.reshape(B, S, RG_HEADS, RG_BLOCK)
    gates = jnp.einsum('bshi,hio->bsho', xh, w_gate) + b_gate
    r = jax.nn.sigmoid(gates[..., :RG_BLOCK].astype(jnp.float32)).reshape(B, S, D_RNN)
    i = jax.nn.sigmoid(gates[..., RG_BLOCK:].astype(jnp.float32)).reshape(B, S, D_RNN)
    log_a = -RG_C * r * jax.nn.softplus(-lam.astype(jnp.float32))
    a = jnp.exp(log_a)
    mult = jnp.sqrt(-jnp.expm1(2.0 * log_a))
    bterm = mult * (i * xc.astype(jnp.float32))
    _, h = lax.associative_scan(_lru_combine, (a, bterm), axis=1)
    y = h.astype(xn.dtype) * gate_branch
    return y @ w_out


def _mem_attention(xn, memn, wq, wkv, wo):
    B, S, D = xn.shape
    M = memn.shape[1]
    q = (xn @ wq).reshape(B, S, XA_HEADS, XA_HEAD_DIM)
    kv = memn @ wkv
    k = kv[..., :D].reshape(B, M, XA_HEADS, XA_HEAD_DIM)
    v = kv[..., D:].reshape(B, M, XA_HEADS, XA_HEAD_DIM)
    s = jnp.einsum('bshd,bmhd->bhsm', q, k).astype(jnp.float32) * (XA_HEAD_DIM ** -0.5)
    p = jax.nn.softmax(s, axis=-1).astype(v.dtype)
    o = jnp.einsum('bhsm,bmhd->bshd', p, v).reshape(B, S, D)
    return o @ wo


def _hier_moe(xn, wg, bg, we, be, w_gu, w_down):
    B, S, D = xn.shape
    t = xn.reshape(B * S, D)
    g_prob = jax.nn.softmax((t @ wg + bg).astype(jnp.float32), axis=-1)
    g_p, g_idx = lax.top_k(g_prob, 1)
    e_logits = (jnp.einsum('nd,gde->nge', t, we) + be).astype(jnp.float32)
    e_sel = jnp.take_along_axis(e_logits, g_idx[:, :, None], axis=1)[:, 0]
    e_prob = jax.nn.softmax(e_sel, axis=-1)
    e_p, e_idx = lax.top_k(e_prob, TOP_K_IN_GROUP)
    e_p = e_p / jnp.sum(e_p, axis=-1, keepdims=True)
    gate_w = g_p * e_p
    expert_id = g_idx * EXPERTS_PER_GROUP + e_idx
    combine = jnp.sum(jax.nn.one_hot(expert_id, N_EXPERTS, dtype=jnp.float32)
                      * gate_w[..., None], axis=1)
    out = jnp.zeros((B * S, D), jnp.float32)
    for e in range(N_EXPERTS):
        gu = t @ w_gu[e]
        h = jax.nn.silu(gu[:, :D_EXPERT]) * gu[:, D_EXPERT:]
        out = out + combine[:, e:e + 1] * (h @ w_down[e]).astype(jnp.float32)
    return out.astype(xn.dtype).reshape(B, S, D)


def setup_inputs(seed: int = 0) -> dict:
    key = jax.random.key(seed)
    ks = iter(jax.random.split(key, 32))

    def nrm(shape, scale):
        return jax.random.normal(next(ks), shape, jnp.float32) * scale

    def gain(shape):
        return 1.0 + nrm(shape, 0.02)

    u = jax.random.uniform(next(ks), (N_RG_LAYERS, D_RNN), jnp.float32, 0.9, 0.999)
    sa = u ** (1.0 / RG_C)
    rg_lambda = jnp.log(sa) - jnp.log1p(-sa)
    return {
        "x": nrm((BATCH, SEQ, D_MODEL), 1.0),
        "mem": nrm((BATCH, MEM_LEN, D_MODEL), 1.0),
        "pool_w": nrm((N_POOL_LAYERS, POOL_GROUPS, POOL_GW, POOL_GW), POOL_GW ** -0.5),
        "pool_scale": gain((N_POOL_LAYERS, D_MODEL)),
        "rg_w_in": nrm((N_RG_LAYERS, D_MODEL, 2 * D_RNN), D_MODEL ** -0.5),
        "rg_conv_w": nrm((N_RG_LAYERS, CONV_W, D_RNN), CONV_W ** -0.5),
        "rg_conv_b": nrm((N_RG_LAYERS, D_RNN), 0.01),
        "rg_w_gate": nrm((N_RG_LAYERS, RG_HEADS, RG_BLOCK, 2 * RG_BLOCK), RG_BLOCK ** -0.5),
        "rg_b_gate": nrm((N_RG_LAYERS, RG_HEADS, 2 * RG_BLOCK), 0.01),
        "rg_lambda": rg_lambda,
        "rg_w_out": nrm((N_RG_LAYERS, D_RNN, D_MODEL), D_RNN ** -0.5),
        "xa_wq": nrm((DEPTH, D_MODEL, D_MODEL), D_MODEL ** -0.5),
        "xa_wkv": nrm((DEPTH, D_MODEL, 2 * D_MODEL), D_MODEL ** -0.5),
        "xa_wo": nrm((DEPTH, D_MODEL, D_MODEL), D_MODEL ** -0.5),
        "moe_wg": nrm((DEPTH, D_MODEL, N_GROUPS), D_MODEL ** -0.5),
        "moe_bg": nrm((DEPTH, N_GROUPS), 0.01),
        "moe_we": nrm((DEPTH, N_GROUPS, D_MODEL, EXPERTS_PER_GROUP), D_MODEL ** -0.5),
        "moe_be": nrm((DEPTH, N_GROUPS, EXPERTS_PER_GROUP), 0.01),
        "moe_w_gu": nrm((DEPTH, N_EXPERTS, D_MODEL, 2 * D_EXPERT), D_MODEL ** -0.5),
        "moe_w_down": nrm((DEPTH, N_EXPERTS, D_EXPERT, D_MODEL), D_EXPERT ** -0.5),
        "norm_mix": gain((DEPTH, D_MODEL)),
        "norm_xattn": gain((DEPTH, D_MODEL)),
        "norm_mem": gain((DEPTH, D_MODEL)),
        "norm_moe": gain((DEPTH, D_MODEL)),
        "norm_final": gain((D_MODEL,)),
    }


def reference(x, mem, pool_w, pool_scale, rg_w_in, rg_conv_w, rg_conv_b, rg_w_gate, rg_b_gate,
              rg_lambda, rg_w_out, xa_wq, xa_wkv, xa_wo, moe_wg, moe_bg, moe_we, moe_be,
              moe_w_gu, moe_w_down, norm_mix, norm_xattn, norm_mem, norm_moe, norm_final):
    h = x
    for i in range(DEPTH):
        j = i // N_MIXERS
        xn = _rms_norm(h, norm_mix[i])
        if i % N_MIXERS == 0:
            h = h + _pool_mixer(xn, pool_w[j], pool_scale[j])
        else:
            h = h + _rglru_block(xn, rg_w_in[j], rg_conv_w[j], rg_conv_b[j], rg_w_gate[j],
                                 rg_b_gate[j], rg_lambda[j], rg_w_out[j])
        h = h + _mem_attention(_rms_norm(h, norm_xattn[i]), _rms_norm(mem, norm_mem[i]),
                               xa_wq[i], xa_wkv[i], xa_wo[i])
        h = h + _hier_moe(_rms_norm(h, norm_moe[i]), moe_wg[i], moe_bg[i], moe_we[i], moe_be[i],
                          moe_w_gu[i], moe_w_down[i])
    return _rms_norm(h, norm_final)
```

```python
import functools

import jax
import jax.numpy as jnp
from jax import lax
from jax.experimental import pallas as pl
from jax.experimental.pallas import tpu as pltpu

D = 1024
N_HEADS = 4
HEAD = D // N_HEADS
POOL_WINDOWS = (2, 4, 8, 16)
POOL_HALO = 16
CONV_W = 4
RG_C = 8.0
N_GROUPS = 4
EPG = 4
N_EXPERTS = N_GROUPS * EPG
D_EXPERT = D // 2
PAIRS = EPG * (EPG - 1) // 2
N_BUCKETS = N_GROUPS * PAIRS
RMS_EPS = 1e-6
LANES = 128
SUBLANES = 8
DX = D + LANES
VMEM_LIMIT = 48 * 1024 * 1024

TS_POOL = 512
TS_ATT = 512
TS_RG = 256
SCAN_SEGS = SUBLANES
SCAN_PAD = SUBLANES
T_PLAN = 512
T_MOVE = 256
TM = 512

BF16 = jnp.bfloat16
F32 = jnp.float32


def _rms(x, g):
    return x * lax.rsqrt(jnp.mean(x * x, axis=-1, keepdims=True) + RMS_EPS) * g


def _dot(a, b):
    return jnp.dot(a, b, preferred_element_type=F32)


def _cparams(*sem):
    return pltpu.CompilerParams(dimension_semantics=sem, vmem_limit_bytes=VMEM_LIMIT)


def _pool_kernel(h_ref, halo_ref, g_ref, w_ref, sc_ref, o_ref):
    t = pl.program_id(1)
    ts = h_ref.shape[1]
    g = g_ref[...]
    x = h_ref[0]
    xn = _rms(x, g)
    halo = jnp.where(t > 0, _rms(halo_ref[0], g), 0.0)
    ext = jnp.concatenate([halo, xn], axis=0)
    sums = []
    cur = ext
    shift = 1
    for gi in range(len(POOL_WINDOWS)):
        cur = cur + pltpu.roll(cur, shift, 0)
        sums.append(cur[POOL_HALO:, :HEAD])
        cur = cur[:, HEAD:]
        shift *= 2
    pos = t * ts + lax.broadcasted_iota(jnp.int32, (ts, 1), 0) + 1
    outs = []
    for gi, w in enumerate(POOL_WINDOWS):
        cnt = jnp.minimum(pos, w).astype(F32)
        p = sums[gi] / cnt - xn[:, gi * HEAD:(gi + 1) * HEAD]
        outs.append(_dot(p.astype(BF16), w_ref[gi]))
    y = jnp.concatenate(outs, axis=-1) * sc_ref[...]
    o_ref[0] = x + y


def _pool_layer(h, g, w, scale):
    b, s, _ = h.shape
    ts = TS_POOL
    per = ts // POOL_HALO
    return pl.pallas_call(
        _pool_kernel,
        grid=(b, s // ts),
        in_specs=[
            pl.BlockSpec((1, ts, D), lambda i, j: (i, j, 0)),
            pl.BlockSpec((1, POOL_HALO, D), lambda i, j: (i, jnp.maximum(j * per - 1, 0), 0)),
            pl.BlockSpec((1, D), lambda i, j: (0, 0)),
            pl.BlockSpec((len(POOL_WINDOWS), HEAD, HEAD), lambda i, j: (0, 0, 0)),
            pl.BlockSpec((1, D), lambda i, j: (0, 0)),
        ],
        out_specs=pl.BlockSpec((1, ts, D), lambda i, j: (i, j, 0)),
        out_shape=jax.ShapeDtypeStruct(h.shape, F32),
        compiler_params=_cparams("parallel", "parallel"),
        name="pool_mixer",
    )(h, h, g.reshape(1, D), w.astype(BF16), scale.reshape(1, D))


def _gelu_tanh(x):
    c = 0.7978845608028654
    return 0.5 * x * (1.0 + jnp.tanh(c * (x + 0.044715 * (x * x * x))))


def _rglru_kernel(h_ref, g_ref, win_ref, cw_ref, cb_ref, wg_ref, bg_ref, lam_ref, wout_ref,
                  o_ref, xtail, hcarry, a_s, b_s, h_s, p_s):
    t = pl.program_id(1)
    ts = h_ref.shape[1]
    seg = ts // SCAN_SEGS
    pitch = seg + SCAN_PAD

    @pl.when(t == 0)
    def _():
        xtail[...] = jnp.zeros_like(xtail)
        hcarry[...] = jnp.zeros_like(hcarry)

    x = h_ref[0]
    xn = _rms(x, g_ref[...]).astype(BF16)
    proj = _dot(xn, win_ref[...])
    xb = proj[:, :D]
    gate_branch = _gelu_tanh(proj[:, D:])

    ext = jnp.concatenate([xtail[...], xb], axis=0)
    xc = cb_ref[...] + xb * cw_ref[CONV_W - 1:CONV_W, :]
    for k in range(1, CONV_W):
        xc = xc + pltpu.roll(ext, k, 0)[SUBLANES:, :] * cw_ref[CONV_W - 1 - k:CONV_W - k, :]
    xtail[...] = xb[ts - SUBLANES:, :]

    rs, is_ = [], []
    for hd in range(N_HEADS):
        gts = _dot(xc[:, hd * HEAD:(hd + 1) * HEAD].astype(BF16), wg_ref[hd])
        gts = gts + bg_ref[:, hd * 2 * HEAD:(hd + 1) * 2 * HEAD]
        rs.append(jax.nn.sigmoid(gts[:, :HEAD]))
        is_.append(jax.nn.sigmoid(gts[:, HEAD:]))
    r = jnp.concatenate(rs, axis=-1)
    ig = jnp.concatenate(is_, axis=-1)

    z = -lam_ref[...]
    softplus = jnp.maximum(z, 0.0) + jnp.log1p(jnp.exp(-jnp.abs(z)))
    log_a = (-RG_C) * r * softplus
    a = jnp.exp(log_a)
    mult = jnp.sqrt(1.0 - a * a)
    bterm = mult * (ig * xc)

    chunks = D // LANES
    for s in range(SCAN_SEGS):
        for c in range(chunks):
            a_s[c, pl.ds(s * pitch, seg), :] = a[s * seg:(s + 1) * seg, c * LANES:(c + 1) * LANES]
            b_s[c, pl.ds(s * pitch, seg), :] = bterm[s * seg:(s + 1) * seg, c * LANES:(c + 1) * LANES]
    hloc = [jnp.zeros((SCAN_SEGS, LANES), F32)] * chunks
    ploc = [jnp.ones((SCAN_SEGS, LANES), F32)] * chunks
    for j in range(seg):
        rows = pl.ds(j, SCAN_SEGS, stride=pitch)
        for c in range(chunks):
            aj = a_s[c, rows, :]
            hloc[c] = aj * hloc[c] + b_s[c, rows, :]
            ploc[c] = aj * ploc[c]
            h_s[c, rows, :] = hloc[c]
            p_s[c, rows, :] = ploc[c]
    hloc = jnp.concatenate(hloc, axis=-1)
    ploc = jnp.concatenate(ploc, axis=-1)
    carry = hcarry[...]
    hs = []
    for s in range(SCAN_SEGS):
        seg_rows = pl.ds(s * pitch, seg)
        hseg = jnp.concatenate([h_s[c, seg_rows, :] for c in range(chunks)], axis=-1)
        pseg = jnp.concatenate([p_s[c, seg_rows, :] for c in range(chunks)], axis=-1)
        hs.append(hseg + pseg * carry)
        carry = ploc[s:s + 1, :] * carry + hloc[s:s + 1, :]
    hcarry[...] = carry
    hfull = jnp.concatenate(hs, axis=0)

    y = (hfull * gate_branch).astype(BF16)
    o_ref[0] = x + _dot(y, wout_ref[...])


def _rglru_layer(h, g, w_in, conv_w, conv_b, w_gate, b_gate, lam, w_out):
    b, s, _ = h.shape
    ts = TS_RG
    rows = SCAN_SEGS * (ts // SCAN_SEGS + SCAN_PAD)
    const = lambda *shape: pl.BlockSpec(shape, lambda i, j: (0,) * len(shape))
    return pl.pallas_call(
        _rglru_kernel,
        grid=(b, s // ts),
        in_specs=[
            pl.BlockSpec((1, ts, D), lambda i, j: (i, j, 0)),
            const(1, D), const(D, 2 * D), const(CONV_W, D), const(1, D),
            const(N_HEADS, HEAD, 2 * HEAD), const(1, 2 * D), const(1, D), const(D, D),
        ],
        out_specs=pl.BlockSpec((1, ts, D), lambda i, j: (i, j, 0)),
        out_shape=jax.ShapeDtypeStruct(h.shape, F32),
        scratch_shapes=[
            pltpu.VMEM((SUBLANES, D), F32), pltpu.VMEM((1, D), F32),
        ] + [pltpu.VMEM((D // LANES, rows, LANES), F32)] * 4,
        compiler_params=_cparams("parallel", "arbitrary"),
        name="rglru_mixer",
    )(h, g.reshape(1, D), w_in.astype(BF16), conv_w, conv_b.reshape(1, D),
      w_gate.astype(BF16), b_gate.reshape(1, 2 * D), lam.reshape(1, D), w_out.astype(BF16))


def _kv_kernel(mem_ref, g_ref, wkv_ref, k_ref, v_ref):
    mn = _rms(mem_ref[0], g_ref[...]).astype(BF16)
    kv = _dot(mn, wkv_ref[...])
    k_ref[0] = kv[:, :D].astype(BF16)
    v_ref[0] = kv[:, D:].astype(BF16)


def _kv_proj(mem, g, wkv):
    b, m, _ = mem.shape
    return pl.pallas_call(
        _kv_kernel,
        grid=(b,),
        in_specs=[
            pl.BlockSpec((1, m, D), lambda i: (i, 0, 0)),
            pl.BlockSpec((1, D), lambda i: (0, 0)),
            pl.BlockSpec((D, 2 * D), lambda i: (0, 0)),
        ],
        out_specs=[pl.BlockSpec((1, m, D), lambda i: (i, 0, 0))] * 2,
        out_shape=[jax.ShapeDtypeStruct((b, m, D), BF16)] * 2,
        compiler_params=_cparams("parallel"),
        name="kv_proj",
    )(mem, g.reshape(1, D), wkv.astype(BF16))


def _first_index_of_max(vals, lane):
    m = jnp.max(vals, axis=-1, keepdims=True)
    idx = jnp.min(jnp.where(vals == m, lane, float(LANES)), axis=-1, keepdims=True)
    return m, idx


def _route(logits):
    lane = lax.broadcasted_iota(jnp.int32, logits.shape, 1).astype(F32)
    neg = -jnp.inf
    gl = jnp.where(lane < N_GROUPS, logits, neg)
    gmax, gidx = _first_index_of_max(gl, lane)
    g_p = 1.0 / jnp.sum(jnp.exp(gl - gmax), axis=-1, keepdims=True)
    first = N_GROUPS + gidx * EPG
    el = jnp.where((lane >= first) & (lane < first + EPG), logits, neg)
    m1, i1 = _first_index_of_max(el, lane)
    el2 = jnp.where(lane == i1, neg, el)
    m2, i2 = _first_index_of_max(el2, lane)
    e2 = jnp.exp(m2 - m1)
    w1 = 1.0 / (1.0 + e2)
    w2 = e2 * w1
    ea = i1 - first
    eb = i2 - first
    lo = jnp.minimum(ea, eb)
    hi = jnp.maximum(ea, eb)
    g_lo = g_p * jnp.where(ea < eb, w1, w2)
    g_hi = g_p * jnp.where(ea < eb, w2, w1)
    bucket = gidx * PAIRS + (lo * (2 * EPG - 1 - lo)) * 0.5 + (hi - lo - 1)
    return jnp.where(lane == 0, g_lo, jnp.where(lane == 1, g_hi, jnp.where(lane == 2, bucket, 0.0)))


def _xattn_kernel(h_ref, g_ref, wq_ref, k_ref, v_ref, wo_ref, gm_ref, wr_ref, br_ref,
                  o_ref, route_ref):
    x = h_ref[0]
    xn = _rms(x, g_ref[...]).astype(BF16)
    q = (_dot(xn, wq_ref[...]) * (HEAD ** -0.5)).astype(BF16)
    outs = []
    for hd in range(N_HEADS):
        sl = slice(hd * HEAD, (hd + 1) * HEAD)
        s = lax.dot_general(q[:, sl], k_ref[0, :, sl], (((1,), (1,)), ((), ())),
                            preferred_element_type=F32)
        p = jnp.exp(s - jnp.max(s, axis=-1, keepdims=True))
        p = p / jnp.sum(p, axis=-1, keepdims=True)
        outs.append(_dot(p.astype(BF16), v_ref[0, :, sl]))
    o = jnp.concatenate(outs, axis=-1).astype(BF16)
    h2 = x + _dot(o, wo_ref[...])
    o_ref[0] = h2
    xm = _rms(h2, gm_ref[...]).astype(BF16)
    route_ref[...] = _route(_dot(xm, wr_ref[...]) + br_ref[...])


def _xattn_layer(h, k, v, g, wq, wo, g_moe, w_route, b_route):
    b, s, _ = h.shape
    m = k.shape[1]
    ts = TS_ATT
    nt = s // ts
    const = lambda *shape: pl.BlockSpec(shape, lambda i, j: (0,) * len(shape))
    return pl.pallas_call(
        _xattn_kernel,
        grid=(b, nt),
        in_specs=[
            pl.BlockSpec((1, ts, D), lambda i, j: (i, j, 0)),
            const(1, D), const(D, D),
            pl.BlockSpec((1, m, D), lambda i, j: (i, 0, 0)),
            pl.BlockSpec((1, m, D), lambda i, j: (i, 0, 0)),
            const(D, D), const(1, D), const(D, LANES), const(1, LANES),
        ],
        out_specs=[
            pl.BlockSpec((1, ts, D), lambda i, j: (i, j, 0)),
            pl.BlockSpec((ts, LANES), lambda i, j: (i * nt + j, 0)),
        ],
        out_shape=[
            jax.ShapeDtypeStruct(h.shape, F32),
            jax.ShapeDtypeStruct((b * s, LANES), F32),
        ],
        compiler_params=_cparams("parallel", "parallel"),
        name="mem_xattn_router",
    )(h, g.reshape(1, D), wq.astype(BF16), k, v, wo.astype(BF16), g_moe.reshape(1, D),
      w_route, b_route)


def _plan_kernel(route_ref, pos_ref, cnt_ref, carry, starts):
    ph = pl.program_id(0)
    i = pl.program_id(1)
    tp = route_ref.shape[0]
    lane = lax.broadcasted_iota(jnp.int32, (tp, LANES), 1).astype(F32)
    onehot = lane == route_ref[:, 2:3]
    colsum = jnp.sum(onehot.astype(F32), axis=0, keepdims=True)

    @pl.when((ph == 0) & (i == 0))
    def _():
        carry[...] = jnp.zeros_like(carry)

    @pl.when(ph == 0)
    def _():
        carry[...] += colsum

    @pl.when((ph == 1) & (i == 0))
    def _():
        cnt = carry[...]
        cnt_ref[...] = cnt
        tiles = jnp.floor((cnt + (TM - 1)) * (1.0 / TM))
        r = lax.broadcasted_iota(jnp.int32, (LANES, LANES), 0)
        c = lax.broadcasted_iota(jnp.int32, (LANES, LANES), 1)
        upper = jnp.where(r < c, 1.0, 0.0).astype(BF16)
        tiles8 = jnp.broadcast_to(tiles, (SUBLANES, LANES)).astype(BF16)
        starts[...] = _dot(tiles8, upper)[0:1, :] * TM
        carry[...] = jnp.zeros_like(carry)

    @pl.when(ph == 1)
    def _():
        r = lax.broadcasted_iota(jnp.int32, (tp, tp), 0)
        c = lax.broadcasted_iota(jnp.int32, (tp, tp), 1)
        lower = jnp.where(c < r, 1.0, 0.0).astype(BF16)
        before = _dot(lower, jnp.where(onehot, 1.0, 0.0).astype(BF16))
        dest = jnp.where(onehot, before + carry[...] + starts[...], 0.0)
        pos_ref[...] = jnp.sum(dest, axis=-1, keepdims=True).astype(jnp.int32)
        carry[...] += colsum


def _plan(route):
    n = route.shape[0]
    tp = T_PLAN
    return pl.pallas_call(
        _plan_kernel,
        grid=(2, n // tp),
        in_specs=[pl.BlockSpec((tp, LANES), lambda p, i: (i, 0))],
        out_specs=[
            pl.BlockSpec((tp, 1), lambda p, i: (i * p, 0)),
            pl.BlockSpec((1, LANES), lambda p, i: (0, 0)),
        ],
        out_shape=[
            jax.ShapeDtypeStruct((n, 1), jnp.int32),
            jax.ShapeDtypeStruct((1, LANES), F32),
        ],
        scratch_shapes=[pltpu.VMEM((1, LANES), F32), pltpu.VMEM((1, LANES), F32)],
        compiler_params=_cparams("arbitrary", "arbitrary"),
        name="moe_plan",
    )(route)


def _row_copy(src, src_row, dst, dst_row, sem):
    return pltpu.make_async_copy(src.at[pl.ds(src_row, 1), :], dst.at[pl.ds(dst_row, 1), :], sem)


def _dispatch_kernel(pos_ref, h_ref, route_ref, g_ref, xs_in_ref, xs_ref, buf, sem):
    del xs_in_ref
    tm = h_ref.shape[0]
    buf[:, :D] = _rms(h_ref[...], g_ref[...])
    buf[:, D:] = route_ref[...]

    def issue(r, c):
        _row_copy(buf, r, xs_ref, pos_ref[0, 0, r], sem).start()
        return c

    lax.fori_loop(0, tm, issue, 0)

    def drain(r, c):
        _row_copy(buf, r, xs_ref, pos_ref[0, 0, r], sem).wait()
        return c

    lax.fori_loop(0, tm, drain, 0)


def _dispatch(h2d, route, pos3, g, n_rows):
    n = h2d.shape[0]
    tm = T_MOVE
    return pl.pallas_call(
        _dispatch_kernel,
        grid=(n // tm,),
        in_specs=[
            pl.BlockSpec((1, 1, tm), lambda i: (i, 0, 0), memory_space=pltpu.SMEM),
            pl.BlockSpec((tm, D), lambda i: (i, 0)),
            pl.BlockSpec((tm, LANES), lambda i: (i, 0)),
            pl.BlockSpec((1, D), lambda i: (0, 0)),
            pl.BlockSpec(memory_space=pl.ANY),
        ],
        out_specs=pl.BlockSpec(memory_space=pl.ANY),
        out_shape=jax.ShapeDtypeStruct((n_rows, DX), F32),
        scratch_shapes=[pltpu.VMEM((tm, DX), F32), pltpu.SemaphoreType.DMA],
        input_output_aliases={4: 0},
        compiler_params=_cparams("arbitrary"),
        name="moe_dispatch",
    )(pos3, h2d, route, g.reshape(1, D), jnp.zeros((n_rows, DX), F32))


def _expert_kernel(tlo_ref, thi_ref, nused_ref, xs_ref, gu_lo_ref, gu_hi_ref, dn_lo_ref, dn_hi_ref,
                   ys_ref):
    del tlo_ref, thi_ref
    used = pl.program_id(0) < nused_ref[0]

    @pl.when(jnp.logical_not(used))
    def _():
        ys_ref[...] = jnp.zeros_like(ys_ref)

    @pl.when(used)
    def _():
        x = xs_ref[:, :D].astype(BF16)
        gates = xs_ref[:, D:]
        y = None
        for col, gu_ref, dn_ref in ((0, gu_lo_ref, dn_lo_ref), (1, gu_hi_ref, dn_hi_ref)):
            gu = _dot(x, gu_ref[...])
            hmid = (jax.nn.silu(gu[:, :D_EXPERT]) * gu[:, D_EXPERT:]).astype(BF16)
            part = gates[:, col:col + 1] * _dot(hmid, dn_ref[...])
            y = part if y is None else y + part
        ys_ref[...] = y


def _experts(xs, w_gu, w_down, tlo, thi, nused):
    n_rows = xs.shape[0]
    n_tiles = n_rows // TM
    row = lambda i, tlo, thi, nu: (jnp.minimum(i, nu[0] - 1), 0)
    grid_spec = pltpu.PrefetchScalarGridSpec(
        num_scalar_prefetch=3,
        grid=(n_tiles,),
        in_specs=[
            pl.BlockSpec((TM, DX), row),
            pl.BlockSpec((None, D, 2 * D_EXPERT), lambda i, tlo, thi, nu: (tlo[i], 0, 0)),
            pl.BlockSpec((None, D, 2 * D_EXPERT), lambda i, tlo, thi, nu: (thi[i], 0, 0)),
            pl.BlockSpec((None, D_EXPERT, D), lambda i, tlo, thi, nu: (tlo[i], 0, 0)),
            pl.BlockSpec((None, D_EXPERT, D), lambda i, tlo, thi, nu: (thi[i], 0, 0)),
        ],
        out_specs=pl.BlockSpec((TM, D), lambda i, tlo, thi, nu: (i, 0)),
    )
    return pl.pallas_call(
        _expert_kernel,
        grid_spec=grid_spec,
        out_shape=jax.ShapeDtypeStruct((n_rows, D), F32),
        compiler_params=_cparams("arbitrary"),
        name="moe_experts",
    )(tlo, thi, nused, xs, w_gu, w_gu, w_down, w_down)


def _combine_kernel(pos_ref, h_ref, ys_ref, gf_ref, o_ref, buf, sem, *, final_norm):
    tm = h_ref.shape[0]

    def issue(r, c):
        _row_copy(ys_ref, pos_ref[0, 0, r], buf, r, sem).start()
        return c

    lax.fori_loop(0, tm, issue, 0)

    def drain(r, c):
        _row_copy(ys_ref, pos_ref[0, 0, r], buf, r, sem).wait()
        return c

    lax.fori_loop(0, tm, drain, 0)
    out = h_ref[...] + buf[...]
    if final_norm:
        out = _rms(out, gf_ref[...])
    o_ref[...] = out


def _combine(h2d, ys, pos3, g_final, final_norm):
    n = h2d.shape[0]
    tm = T_MOVE
    return pl.pallas_call(
        functools.partial(_combine_kernel, final_norm=final_norm),
        grid=(n // tm,),
        in_specs=[
            pl.BlockSpec((1, 1, tm), lambda i: (i, 0, 0), memory_space=pltpu.SMEM),
            pl.BlockSpec((tm, D), lambda i: (i, 0)),
            pl.BlockSpec(memory_space=pl.ANY),
            pl.BlockSpec((1, D), lambda i: (0, 0)),
        ],
        out_specs=pl.BlockSpec((tm, D), lambda i: (i, 0)),
        out_shape=jax.ShapeDtypeStruct((n, D), F32),
        scratch_shapes=[pltpu.VMEM((tm, D), F32), pltpu.SemaphoreType.DMA],
        compiler_params=_cparams("arbitrary"),
        name="moe_combine",
    )(pos3, h2d, ys, g_final.reshape(1, D))


def _bucket_experts():
    lo, hi = [], []
    for g in range(N_GROUPS):
        for a in range(EPG):
            for b in range(a + 1, EPG):
                lo.append(g * EPG + a)
                hi.append(g * EPG + b)
    return jnp.array(lo, jnp.int32), jnp.array(hi, jnp.int32)


def _moe_layer(h2d, route, g_moe, w_gu, w_down, g_final, final_norm):
    n = h2d.shape[0]
    n_tiles = n // TM + N_BUCKETS
    pos, cnt = _plan(route)
    counts = cnt[0, :N_BUCKETS].astype(jnp.int32)
    tile_end = jnp.cumsum((counts + (TM - 1)) // TM)
    nused = tile_end[-1:]
    tile_idx = jnp.minimum(jnp.arange(n_tiles), nused[0] - 1)
    tile_bucket = jnp.sum(tile_end[None, :] <= tile_idx[:, None], axis=1).astype(jnp.int32)
    b_lo, b_hi = _bucket_experts()
    pos3 = pos.reshape(n // T_MOVE, 1, T_MOVE)
    xs = _dispatch(h2d, route, pos3, g_moe, n_tiles * TM)
    ys = _experts(xs, w_gu.astype(BF16), w_down.astype(BF16),
                  b_lo[tile_bucket], b_hi[tile_bucket], nused.astype(jnp.int32))
    return _combine(h2d, ys, pos3, g_final, final_norm)


def _router_params(wg, bg, we, be):
    w = jnp.concatenate([wg, jnp.transpose(we, (1, 0, 2)).reshape(D, N_EXPERTS)], axis=1)
    b = jnp.concatenate([bg, be.reshape(N_EXPERTS)])
    pad = LANES - w.shape[1]
    return jnp.pad(w, ((0, 0), (0, pad))).astype(BF16), jnp.pad(b, (0, pad)).reshape(1, LANES)


@jax.jit
def kernel(x, mem, pool_w, pool_scale, rg_w_in, rg_conv_w, rg_conv_b, rg_w_gate, rg_b_gate, rg_lambda, rg_w_out, xa_wq, xa_wkv, xa_wo, moe_wg, moe_bg, moe_we, moe_be, moe_w_gu, moe_w_down, norm_mix, norm_xattn, norm_mem, norm_moe, norm_final):
    b, s, _ = x.shape
    depth = norm_mix.shape[0]
    h = x
    for i in range(depth):
        j = i // 2
        if i % 2 == 0:
            h = _pool_layer(h, norm_mix[i], pool_w[j], pool_scale[j])
        else:
            h = _rglru_layer(h, norm_mix[i], rg_w_in[j], rg_conv_w[j], rg_conv_b[j], rg_w_gate[j],
                             rg_b_gate[j], rg_lambda[j], rg_w_out[j])
        k, v = _kv_proj(mem, norm_mem[i], xa_wkv[i])
        w_route, b_route = _router_params(moe_wg[i], moe_bg[i], moe_we[i], moe_be[i])
        h, route = _xattn_layer(h, k, v, norm_xattn[i], xa_wq[i], xa_wo[i], norm_moe[i],
                                w_route, b_route)
        h = _moe_layer(h.reshape(b * s, D), route, norm_moe[i], moe_w_gu[i], moe_w_down[i],
                       norm_final, i == depth - 1).reshape(b, s, D)
    return h
```
